```python
import jax, jax.numpy as jnp
from jax import lax
import numpy as np

D_MODEL = 1024
BATCH = 8
SEQ = 4096
DEPTH = 2

MLA_HEADS = 4
MLA_Q_RANK = 256
MLA_KV_RANK = 128
MLA_NOPE = 64
MLA_ROPE = 32
MLA_V = 64
MLA_Q_BLOCK = 128
SC_WIDTH = 256
SC_KERNEL = 3
CF_WIDTH = 256
CF_KERNEL = 31
GQ_HEADS = 4
GQ_KV_HEADS = 2
GQ_HEAD_DIM = 64
WINDOW = 128
BLOCK = 128
ROPE_THETA = 10000.0
N_EXPERTS = 16
EXPERT_FF = 1024
EC_CAPACITY_FACTOR = 2
LN_EPS = 1e-5
RMS_EPS = 1e-6
DEEPNORM_ALPHA = (2 * DEPTH) ** 0.25
DEEPNORM_BETA = (8 * DEPTH) ** -0.25
NEG_INF = -1e30

IN_SPLITS = (MLA_Q_RANK, MLA_KV_RANK, MLA_ROPE,
             SC_WIDTH, SC_WIDTH, SC_WIDTH,
             CF_WIDTH, CF_WIDTH,
             GQ_HEADS * GQ_HEAD_DIM, GQ_KV_HEADS * GQ_HEAD_DIM, GQ_KV_HEADS * GQ_HEAD_DIM)
D_IN = sum(IN_SPLITS)
D_MIX = MLA_HEADS * MLA_V + SC_WIDTH + CF_WIDTH + GQ_HEADS * GQ_HEAD_DIM

kernel_name = "hybrid_parallel_mla_conv_swa_ecmoe_encoder"


def layer_norm(x, g, b):
    xf = x.astype(jnp.float32)
    mu = jnp.mean(xf, axis=-1, keepdims=True)
    var = jnp.mean(jnp.square(xf - mu), axis=-1, keepdims=True)
    y = (xf - mu) * lax.rsqrt(var + LN_EPS) * g.astype(jnp.float32) + b.astype(jnp.float32)
    return y.astype(x.dtype)


def rms_norm(x, g):
    xf = x.astype(jnp.float32)
    y = xf * lax.rsqrt(jnp.mean(jnp.square(xf), axis=-1, keepdims=True) + RMS_EPS) * g.astype(jnp.float32)
    return y.astype(x.dtype)


def rope_cos_sin(positions, dim):
    inv = 1.0 / (ROPE_THETA ** (jnp.arange(0, dim, 2, dtype=jnp.float32) / dim))
    ang = positions.astype(jnp.float32)[..., None] * inv
    return jnp.cos(ang), jnp.sin(ang)


def apply_rope(x, cos, sin):
    xf = x.astype(jnp.float32)
    x1, x2 = jnp.split(xf, 2, axis=-1)
    c = cos[:, :, None, :]
    s = sin[:, :, None, :]
    return jnp.concatenate([x1 * c - x2 * s, x2 * c + x1 * s], axis=-1).astype(x.dtype)


def depthwise_conv(x, w):
    k = w.shape[0]
    c = x.shape[-1]
    return lax.conv_general_dilated(
        x, w[:, None, :].astype(x.dtype), window_strides=(1,),
        padding=[(k // 2, k // 2)], dimension_numbers=('NWC', 'WIO', 'NWC'),
        feature_group_count=c)


def mla_attention(q_lat, kv_lat, k_rope, q_norm_g, w_uq, kv_norm_g, w_ukv, cos, sin):
    b, s, _ = q_lat.shape
    h = MLA_HEADS
    dqk = MLA_NOPE + MLA_ROPE
    q = (rms_norm(q_lat, q_norm_g) @ w_uq).reshape(b, s, h, dqk)
    q = jnp.concatenate([q[..., :MLA_NOPE], apply_rope(q[..., MLA_NOPE:], cos, sin)], axis=-1)
    kv = (rms_norm(kv_lat, kv_norm_g) @ w_ukv).reshape(b, s, h, MLA_NOPE + MLA_V)
    k_nope, v = kv[..., :MLA_NOPE], kv[..., MLA_NOPE:]
    k_pe = apply_rope(k_rope[:, :, None, :], cos, sin)
    k = jnp.concatenate([k_nope, jnp.broadcast_to(k_pe, (b, s, h, MLA_ROPE))], axis=-1)
    scale = dqk ** -0.5
    nb = s // MLA_Q_BLOCK
    q_blocks = q.reshape(b, nb, MLA_Q_BLOCK, h, dqk).transpose(1, 0, 2, 3, 4)

    def attend(qb):
        sc = jnp.einsum('bqhd,bkhd->bhqk', qb, k).astype(jnp.float32) * scale
        p = jax.nn.softmax(sc, axis=-1).astype(v.dtype)
        return jnp.einsum('bhqk,bkhd->bqhd', p, v)

    o = lax.map(attend, q_blocks)
    return o.transpose(1, 0, 2, 3, 4).reshape(b, s, h * MLA_V)


def window_gqa(q, k, v, sink, cos, sin):
    b, s, _ = q.shape
    h, kvh, d = GQ_HEADS, GQ_KV_HEADS, GQ_HEAD_DIM
    g = h // kvh
    nb = s // BLOCK
    q = apply_rope(q.reshape(b, s, h, d), cos, sin).reshape(b, nb, BLOCK, kvh, g, d)
    k = apply_rope(k.reshape(b, s, kvh, d), cos, sin)
    v = v.reshape(b, s, kvh, d)
    pad = ((0, 0), (BLOCK, BLOCK), (0, 0), (0, 0))
    kp = jnp.pad(k, pad).reshape(b, nb + 2, BLOCK, kvh, d)
    vp = jnp.pad(v, pad).reshape(b, nb + 2, BLOCK, kvh, d)
    kb = jnp.concatenate([kp[:, :-2], kp[:, 1:-1], kp[:, 2:]], axis=2)
    vb = jnp.concatenate([vp[:, :-2], vp[:, 1:-1], vp[:, 2:]], axis=2)
    sc = jnp.einsum('bnqkgd,bnjkd->bnkgqj', q, kb).astype(jnp.float32) * (d ** -0.5)
    blk = jnp.arange(nb)[:, None]
    q_pos = blk * BLOCK + jnp.arange(BLOCK)[None, :]
    k_pos = (blk - 1) * BLOCK + jnp.arange(3 * BLOCK)[None, :]
    valid = ((jnp.abs(q_pos[:, :, None] - k_pos[:, None, :]) <= WINDOW)
             & (k_pos[:, None, :] >= 0) & (k_pos[:, None, :] < s))
    sc = jnp.where(valid[None, :, None, None], sc, NEG_INF)
    sink_l = jnp.broadcast_to(sink.astype(jnp.float32).reshape(kvh, g, 1, 1), sc.shape[:-1] + (1,))
    p = jax.nn.softmax(jnp.concatenate([sc, sink_l], axis=-1), axis=-1)[..., :-1].astype(v.dtype)
    o = jnp.einsum('bnkgqj,bnjkd->bnqkgd', p, vb)
    return o.reshape(b, s, h * d)


def conformer_conv(a, gate, w_dw, b_dw, g_n, b_n):
    u = a * jax.nn.sigmoid(gate)
    u = depthwise_conv(u, w_dw) + b_dw.astype(u.dtype)
    return jax.nn.silu(layer_norm(u, g_n, b_n))


def expert_choice_moe(x, w_router, b_router, w_gate, w_up, w_down):
    b, s, d = x.shape
    cap = EC_CAPACITY_FACTOR * s // N_EXPERTS
    logits = (x @ w_router).astype(jnp.float32) + b_router.astype(jnp.float32)
    aff = jax.nn.softmax(logits, axis=-1)
    gates, idx = lax.top_k(aff.transpose(0, 2, 1), cap)
    xs = jax.vmap(lambda xb, ib: xb[ib])(x, idx)
    hid = jax.nn.silu(jnp.einsum('becd,edf->becf', xs, w_gate)) * jnp.einsum('becd,edf->becf', xs, w_up)
    o = jnp.einsum('becf,efd->becd', hid, w_down) * gates[..., None].astype(x.dtype)
    y = jax.vmap(lambda ib, ob: jnp.zeros((s, d), ob.dtype).at[ib.reshape(-1)].add(ob.reshape(-1, d)))(idx, o)
    return y


def _normal(k, shape, scale):
    return jax.random.normal(k, shape, jnp.float32) * scale


def setup_inputs(seed: int = 0) -> dict:
    key = jax.random.key(seed)
    ks = jax.random.split(key, 24)
    L = DEPTH
    x = jax.random.normal(ks[0], (BATCH, SEQ, D_MODEL), jnp.float32)
    offset = jax.random.randint(ks[1], (BATCH, 1), 0, 1024, dtype=jnp.int32)
    positions = (offset + jnp.arange(SEQ, dtype=jnp.int32)[None, :]).astype(jnp.int32)
    return {
        "x": x,
        "positions": positions,
        "w_in": _normal(ks[2], (L, D_MODEL, D_IN), D_MODEL ** -0.5),
        "q_norm_g": 1.0 + _normal(ks[3], (L, MLA_Q_RANK), 0.02),
        "w_uq": _normal(ks[4], (L, MLA_Q_RANK, MLA_HEADS * (MLA_NOPE + MLA_ROPE)), MLA_Q_RANK ** -0.5),
        "kv_norm_g": 1.0 + _normal(ks[5], (L, MLA_KV_RANK), 0.02),
        "w_ukv": _normal(ks[6], (L, MLA_KV_RANK, MLA_HEADS * (MLA_NOPE + MLA_V)), MLA_KV_RANK ** -0.5),
        "sconv_w": _normal(ks[7], (L, SC_KERNEL, SC_WIDTH), SC_KERNEL ** -0.5),
        "cconv_w": _normal(ks[8], (L, CF_KERNEL, CF_WIDTH), CF_KERNEL ** -0.5),
        "cconv_b": _normal(ks[9], (L, CF_WIDTH), 0.02),
        "cnorm_g": 1.0 + _normal(ks[10], (L, CF_WIDTH), 0.02),
        "cnorm_b": _normal(ks[11], (L, CF_WIDTH), 0.02),
        "sink": _normal(ks[12], (L, GQ_HEADS), 0.5),
        "w_out": _normal(ks[13], (L, D_MIX, D_MODEL), DEEPNORM_BETA * D_MIX ** -0.5),
        "ln1_g": 1.0 + _normal(ks[14], (L, D_MODEL), 0.02),
        "ln1_b": _normal(ks[15], (L, D_MODEL), 0.02),
        "w_router": _normal(ks[16], (L, D_MODEL, N_EXPERTS), D_MODEL ** -0.5),
        "b_router": _normal(ks[17], (L, N_EXPERTS), 0.01),
        "w_gate": _normal(ks[18], (L, N_EXPERTS, D_MODEL, EXPERT_FF), D_MODEL ** -0.5),
        "w_up": _normal(ks[19], (L, N_EXPERTS, D_MODEL, EXPERT_FF), D_MODEL ** -0.5),
        "w_down": _normal(ks[20], (L, N_EXPERTS, EXPERT_FF, D_MODEL), DEEPNORM_BETA * EXPERT_FF ** -0.5),
        "ln2_g": 1.0 + _normal(ks[21], (L, D_MODEL), 0.02),
        "ln2_b": _normal(ks[22], (L, D_MODEL), 0.02),
    }


def reference(x, positions, w_in, q_norm_g, w_uq, kv_norm_g, w_ukv, sconv_w, cconv_w, cconv_b,
              cnorm_g, cnorm_b, sink, w_out, ln1_g, ln1_b, w_router, b_router, w_gate, w_up,
              w_down, ln2_g, ln2_b):
    cos_a, sin_a = rope_cos_sin(positions, MLA_ROPE)
    cos_d, sin_d = rope_cos_sin(positions, GQ_HEAD_DIM)
    split_points = np.cumsum(IN_SPLITS)[:-1].tolist()
    for l in range(DEPTH):
        proj = x @ w_in[l]
        (q_lat, kv_lat, k_rope, sc_b, sc_c, sc_h, cf_a, cf_g,
         gq_q, gq_k, gq_v) = jnp.split(proj, split_points, axis=-1)
        y_a = mla_attention(q_lat, kv_lat, k_rope, q_norm_g[l], w_uq[l], kv_norm_g[l], w_ukv[l], cos_a, sin_a)
        y_b = sc_b * depthwise_conv(sc_c * sc_h, sconv_w[l])
        y_c = conformer_conv(cf_a, cf_g, cconv_w[l], cconv_b[l], cnorm_g[l], cnorm_b[l])
        y_d = window_gqa(gq_q, gq_k, gq_v, sink[l], cos_d, sin_d)
        mix = jnp.concatenate([y_a, y_b, y_c, y_d], axis=-1) @ w_out[l]
        x = layer_norm(DEEPNORM_ALPHA * x + mix, ln1_g[l], ln1_b[l])
        moe = expert_choice_moe(x, w_router[l], b_router[l], w_gate[l], w_up[l], w_down[l])
        x = layer_norm(DEEPNORM_ALPHA * x + moe, ln2_g[l], ln2_b[l])
    return x
```

```python
import functools

import jax
import jax.numpy as jnp
import numpy as np
from jax import lax
from jax.experimental import pallas as pl
from jax.experimental.pallas import tpu as pltpu

F32 = jnp.float32
BF16 = jnp.bfloat16
I32 = jnp.int32

D_MODEL = 1024
MLA_HEADS = 4
MLA_Q_RANK = 256
MLA_KV_RANK = 128
MLA_NOPE = 64
MLA_ROPE = 32
MLA_V = 64
SC_WIDTH = 256
SC_KERNEL = 3
CF_WIDTH = 256
CF_KERNEL = 31
GQ_HEADS = 4
GQ_KV_HEADS = 2
GQ_HEAD_DIM = 64
WINDOW = 128
ROPE_THETA = 10000.0
N_EXPERTS = 16
EXPERT_FF = 1024
EC_CAPACITY_FACTOR = 2
LN_EPS = 1e-5
RMS_EPS = 1e-6
NEG_INF = -1e30
MIN_NORMAL_BITS = 0x00800000

LANES = 128
SUBLANES = 8
VMEM_LIMIT_BYTES = 58 * 1024 * 1024

_C_QLAT = 0
_C_KVLAT = _C_QLAT + MLA_Q_RANK
_C_KROPE = _C_KVLAT + MLA_KV_RANK
_C_SCB = _C_KROPE + LANES
_C_SCC = _C_SCB + SC_WIDTH
_C_SCH = _C_SCC + SC_WIDTH
_C_CFA = _C_SCH + SC_WIDTH
_C_CFG = _C_CFA + CF_WIDTH
_C_GQQ = _C_CFG + CF_WIDTH
_C_GQK = _C_GQQ + GQ_HEADS * GQ_HEAD_DIM
_C_GQV = _C_GQK + 2 * GQ_KV_HEADS * GQ_HEAD_DIM
_C_END = _C_GQV + 2 * GQ_KV_HEADS * GQ_HEAD_DIM

TOKEN_TILE = 512
CONV_TILE = 512
CONV_ROWS = 64
CF_HALO = 16
SC_HALO = 8
MLA_Q_TILE = 256
GQ_Q_TILE = 256
MOE_PITCH_PAD = 8


def _layer_norm(x, g, b):
    mu = jnp.mean(x, axis=-1, keepdims=True)
    xc = x - mu
    var = jnp.mean(xc * xc, axis=-1, keepdims=True)
    return xc * lax.rsqrt(var + LN_EPS) * g + b


def _rms_norm(x, g):
    return x * lax.rsqrt(jnp.mean(x * x, axis=-1, keepdims=True) + RMS_EPS) * g


def _dot(a, b):
    return jnp.dot(a, b, preferred_element_type=F32)


def _dot_nt(a, b):
    return lax.dot_general(a, b, (((1,), (1,)), ((), ())), preferred_element_type=F32)


def _rope(x, cos, sin, group, start, half):
    w = x.shape[-1]
    lane = lax.broadcasted_iota(I32, x.shape, 1) % group
    first_half = (lane >= start) & (lane < start + half)
    partner = jnp.where(first_half, pltpu.roll(x, w - half, 1), pltpu.roll(x, half, 1))
    return x * cos + partner * sin


def _rope_table_kernel(pos_ref, inv_ref, sign_ref, cos_ref, sin_ref):
    ang = pos_ref[...].astype(F32) * inv_ref[...]
    cos_ref[...] = jnp.cos(ang)
    sin_ref[...] = jnp.sin(ang) * sign_ref[...]


def _rope_tables(positions):
    t = positions.size
    inv_a = 1.0 / (ROPE_THETA ** (jnp.arange(0, MLA_ROPE, 2, dtype=F32) / MLA_ROPE))
    inv_d = 1.0 / (ROPE_THETA ** (jnp.arange(0, GQ_HEAD_DIM, 2, dtype=F32) / GQ_HEAD_DIM))
    zeros = lambda n: jnp.zeros((n,), F32)
    ones = lambda n: jnp.ones((n,), F32)
    inv = jnp.concatenate([zeros(MLA_NOPE), inv_a, inv_a, zeros(LANES - MLA_NOPE - MLA_ROPE),
                           inv_d, inv_d, inv_d, inv_d])[None, :]
    sign = jnp.concatenate([zeros(MLA_NOPE), -ones(MLA_ROPE // 2), ones(MLA_ROPE // 2),
                            zeros(LANES - MLA_NOPE - MLA_ROPE),
                            -ones(GQ_HEAD_DIM // 2), ones(GQ_HEAD_DIM // 2),
                            -ones(GQ_HEAD_DIM // 2), ones(GQ_HEAD_DIM // 2)])[None, :]
    width = 2 * LANES
    return pl.pallas_call(
        _rope_table_kernel,
        grid=(t // TOKEN_TILE,),
        in_specs=[pl.BlockSpec((TOKEN_TILE, 1), lambda i: (i, 0)),
                  pl.BlockSpec((1, width), lambda i: (0, 0)),
                  pl.BlockSpec((1, width), lambda i: (0, 0))],
        out_specs=[pl.BlockSpec((TOKEN_TILE, width), lambda i: (i, 0))] * 2,
        out_shape=[jax.ShapeDtypeStruct((t, width), F32)] * 2,
        compiler_params=pltpu.CompilerParams(dimension_semantics=("parallel",)),
        name="rope_tables",
    )(positions.reshape(t, 1), inv, sign)


def _proj_kernel(apply_ln, *refs):
    if apply_ln:
        (x_ref, lng_ref, lnb_ref, win_ref, qg_ref, wuq_ref, kvg_ref, wkv_ref, cos_ref, sin_ref,
         xo_ref, qm_ref, km_ref, vm_ref, scb_ref, usc_ref, ucf_ref, qg_o, kg_o, vg_o) = refs
        x = _layer_norm(x_ref[...], lng_ref[...], lnb_ref[...])
        xo_ref[...] = x
    else:
        (x_ref, win_ref, qg_ref, wuq_ref, kvg_ref, wkv_ref, cos_ref, sin_ref,
         qm_ref, km_ref, vm_ref, scb_ref, usc_ref, ucf_ref, qg_o, kg_o, vg_o) = refs
        x = x_ref[...]
    proj = _dot(x.astype(BF16), win_ref[...])
    tm = proj.shape[0]

    cos_a = cos_ref[:, :LANES]
    sin_a = sin_ref[:, :LANES]
    cos_d = cos_ref[:, LANES:]
    sin_d = sin_ref[:, LANES:]
    tile = lambda a, n: jnp.concatenate([a] * n, axis=1)
    rope_a = functools.partial(_rope, group=LANES, start=MLA_NOPE, half=MLA_ROPE // 2)
    rope_d = functools.partial(_rope, group=GQ_HEAD_DIM, start=0, half=GQ_HEAD_DIM // 2)

    q_lat = _rms_norm(proj[:, _C_QLAT:_C_KVLAT], qg_ref[...])
    q = _dot(q_lat.astype(BF16), wuq_ref[...])
    q = rope_a(q, tile(cos_a, MLA_HEADS), tile(sin_a, MLA_HEADS))
    qm_ref[...] = (q * ((MLA_NOPE + MLA_ROPE) ** -0.5)).astype(BF16)
    kv_lat = _rms_norm(proj[:, _C_KVLAT:_C_KROPE], kvg_ref[...])
    kv = _dot(kv_lat.astype(BF16), wkv_ref[...])
    k_pe = rope_a(proj[:, _C_KROPE:_C_SCB], cos_a, sin_a)
    km_ref[...] = (kv[:, :MLA_HEADS * LANES] + tile(k_pe, MLA_HEADS)).astype(BF16)
    vm_ref[...] = kv[:, MLA_HEADS * LANES:].astype(BF16)

    scb_ref[...] = proj[:, _C_SCB:_C_SCC]
    usc_ref[...] = proj[:, _C_SCC:_C_SCH] * proj[:, _C_SCH:_C_CFA]
    ucf_ref[...] = proj[:, _C_CFA:_C_CFG] * jax.nn.sigmoid(proj[:, _C_CFG:_C_GQQ])

    cos_d2, sin_d2 = tile(cos_d, 2), tile(sin_d, 2)
    gq = rope_d(proj[:, _C_GQQ:_C_GQK], cos_d2, sin_d2)
    qg_o[...] = (gq * (GQ_HEAD_DIM ** -0.5)).astype(BF16)
    gk = rope_d(proj[:, _C_GQK:_C_GQV], cos_d2, sin_d2)
    kg_o[...] = gk.astype(BF16)
    vg_o[...] = proj[:, _C_GQV:_C_END].astype(BF16)


def _proj(x, ln, w, cos_tab, sin_tab):
    t = x.shape[0]
    tm = TOKEN_TILE
    row = lambda width: pl.BlockSpec((tm, width), lambda i: (i, 0))
    full = lambda a: pl.BlockSpec(a.shape, lambda i: (0,) * a.ndim)
    apply_ln = ln is not None
    ins = [x] + (list(ln) if apply_ln else []) + [w["w_in"], w["q_norm_g"], w["w_uq"], w["kv_norm_g"], w["w_kv"],
                                                   cos_tab, sin_tab]
    in_specs = [row(D_MODEL)] + [full(a) for a in ins[1:-2]] + [row(2 * LANES)] * 2
    outs = [(MLA_HEADS * LANES, BF16), (MLA_HEADS * LANES, BF16), (MLA_HEADS * MLA_V, BF16),
            (SC_WIDTH, F32), (SC_WIDTH, F32), (CF_WIDTH, F32),
            (256, BF16), (256, BF16), (256, BF16)]
    if apply_ln:
        outs = [(D_MODEL, F32)] + outs
    res = pl.pallas_call(
        functools.partial(_proj_kernel, apply_ln),
        grid=(t // tm,),
        in_specs=in_specs,
        out_specs=[row(wd) for wd, _ in outs],
        out_shape=[jax.ShapeDtypeStruct((t, wd), dt) for wd, dt in outs],
        compiler_params=pltpu.CompilerParams(dimension_semantics=("parallel",),
                                             vmem_limit_bytes=VMEM_LIMIT_BYTES),
        name="proj",
    )(*ins)
    if apply_ln:
        return res[0], res[1:]
    return x, res


def _conv_kernel(ucf_p, ucf_c, ucf_n, usc_p, usc_c, usc_n, scb_ref, wsc_ref, wcf_ref, bcf_ref, gn_ref, bn_ref,
                 yb_ref, yc_ref, ext_cf, ext_sc):
    i = pl.program_id(1)
    has_prev = (i > 0).astype(F32)
    has_next = (i < pl.num_programs(1) - 1).astype(F32)
    ts = CONV_TILE
    ext_cf[0:CF_HALO, :] = ucf_p[...] * has_prev
    ext_cf[CF_HALO:CF_HALO + ts, :] = ucf_c[...]
    ext_cf[CF_HALO + ts:, :] = ucf_n[...] * has_next
    ext_sc[0:SC_HALO, :] = usc_p[...] * has_prev
    ext_sc[SC_HALO:SC_HALO + ts, :] = usc_c[...]
    ext_sc[SC_HALO + ts:, :] = usc_n[...] * has_next
    for r0 in range(0, ts, CONV_ROWS):
        acc = jnp.zeros((CONV_ROWS, CF_WIDTH), F32)
        for k in range(CF_KERNEL):
            off = r0 + CF_HALO - CF_KERNEL // 2 + k
            acc = acc + wcf_ref[k:k + 1, :] * ext_cf[off:off + CONV_ROWS, :]
        u = _layer_norm(acc + bcf_ref[...], gn_ref[...], bn_ref[...])
        yc_ref[r0:r0 + CONV_ROWS, :] = (u * jax.nn.sigmoid(u)).astype(BF16)
        acc = jnp.zeros((CONV_ROWS, SC_WIDTH), F32)
        for k in range(SC_KERNEL):
            off = r0 + SC_HALO - SC_KERNEL // 2 + k
            acc = acc + wsc_ref[k:k + 1, :] * ext_sc[off:off + CONV_ROWS, :]
        yb_ref[r0:r0 + CONV_ROWS, :] = (scb_ref[r0:r0 + CONV_ROWS, :] * acc).astype(BF16)


def _conv(scb, usc, ucf, w, batch, seq):
    ts = CONV_TILE
    nt = seq // ts
    cur = lambda b, i: (b * nt + i, 0)

    def halo(h, side):
        per_tile, per_seq = ts // h, seq // h
        if side < 0:
            return pl.BlockSpec((h, 256), lambda b, i: (b * per_seq + jnp.maximum(i * per_tile - 1, 0), 0))
        return pl.BlockSpec((h, 256), lambda b, i: (b * per_seq + jnp.minimum((i + 1) * per_tile, per_seq - 1), 0))

    full = lambda a: pl.BlockSpec(a.shape, lambda b, i: (0,) * a.ndim)
    small = [w["sconv_w"], w["cconv_w"], w["cconv_b"], w["cnorm_g"], w["cnorm_b"]]
    t = batch * seq
    return pl.pallas_call(
        _conv_kernel,
        grid=(batch, nt),
        in_specs=[halo(CF_HALO, -1), pl.BlockSpec((ts, 256), cur), halo(CF_HALO, 1),
                  halo(SC_HALO, -1), pl.BlockSpec((ts, 256), cur), halo(SC_HALO, 1),
                  pl.BlockSpec((ts, 256), cur)] + [full(a) for a in small],
        out_specs=[pl.BlockSpec((ts, 256), cur)] * 2,
        out_shape=[jax.ShapeDtypeStruct((t, 256), BF16)] * 2,
        scratch_shapes=[pltpu.VMEM((ts + 2 * CF_HALO, CF_WIDTH), F32), pltpu.VMEM((ts + 2 * SC_HALO, SC_WIDTH), F32)],
        compiler_params=pltpu.CompilerParams(dimension_semantics=("parallel", "parallel")),
        name="conv",
    )(ucf, ucf, ucf, usc, usc, usc, scb, *small)


def _mla_kernel(q_ref, k_ref, v_ref, o_ref):
    lane = lax.broadcasted_iota(I32, o_ref.shape, 1)
    outs = []
    for hh in range(2):
        q = q_ref[:, hh * LANES:(hh + 1) * LANES]
        k = k_ref[:, hh * LANES:(hh + 1) * LANES]
        s = _dot_nt(q, k)
        p = jnp.exp(s - jnp.max(s, axis=-1, keepdims=True))
        denom = jnp.sum(p, axis=-1, keepdims=True)
        outs.append(_dot(p.astype(BF16), v_ref[...]) / denom)
    o_ref[...] = jnp.where(lane < MLA_V, outs[0], outs[1]).astype(BF16)


def _mla(qm, km, vm, batch, seq):
    tq = MLA_Q_TILE
    nq = seq // tq
    return pl.pallas_call(
        _mla_kernel,
        grid=(batch, MLA_HEADS // 2, nq),
        in_specs=[pl.BlockSpec((tq, 2 * LANES), lambda b, h, i: (b * nq + i, h)),
                  pl.BlockSpec((seq, 2 * LANES), lambda b, h, i: (b, h)),
                  pl.BlockSpec((seq, 2 * MLA_V), lambda b, h, i: (b, h))],
        out_specs=pl.BlockSpec((tq, 2 * MLA_V), lambda b, h, i: (b * nq + i, h)),
        out_shape=jax.ShapeDtypeStruct((batch * seq, MLA_HEADS * MLA_V), BF16),
        compiler_params=pltpu.CompilerParams(dimension_semantics=("parallel", "parallel", "parallel"),
                                             vmem_limit_bytes=VMEM_LIMIT_BYTES),
        name="mla_attention",
    )(qm, km, vm)


def _gqa_kernel(sink_ref, q_ref, kp_ref, kc_ref, kn_ref, vp_ref, vc_ref, vn_ref, o_ref):
    i = pl.program_id(1)
    tq = GQ_Q_TILE
    nk = tq + 2 * WINDOW
    k_all = jnp.concatenate([kp_ref[...], kc_ref[...], kn_ref[...]], axis=0)
    v_all = jnp.concatenate([vp_ref[...], vc_ref[...], vn_ref[...]], axis=0)
    r = lax.broadcasted_iota(I32, (tq, nk), 0)
    j = lax.broadcasted_iota(I32, (tq, nk), 1) - WINDOW
    valid = (jnp.abs(r - j) <= WINDOW)
    valid = valid & ((j >= 0) | (i > 0)) & ((j < tq) | (i < pl.num_programs(1) - 1))
    lane = lax.broadcasted_iota(I32, (tq, LANES), 1)
    low = lane < GQ_HEAD_DIM
    for kvh in range(GQ_KV_HEADS):
        q2 = q_ref[:, kvh * LANES:(kvh + 1) * LANES]
        k2 = k_all[:, kvh * LANES:(kvh + 1) * LANES]
        v2 = v_all[:, kvh * LANES:(kvh + 1) * LANES]
        outs = []
        for g in range(GQ_HEADS // GQ_KV_HEADS):
            qh = jnp.where(low if g == 0 else ~low, q2, jnp.zeros_like(q2))
            s = jnp.where(valid, _dot_nt(qh, k2), NEG_INF)
            sink = sink_ref[kvh * (GQ_HEADS // GQ_KV_HEADS) + g]
            m = jnp.maximum(jnp.max(s, axis=-1, keepdims=True), sink)
            p = jnp.exp(s - m)
            denom = jnp.sum(p, axis=-1, keepdims=True) + jnp.exp(sink - m)
            outs.append(_dot((p / denom).astype(BF16), v2))
        o_ref[:, kvh * LANES:(kvh + 1) * LANES] = jnp.where(low, outs[0], outs[1]).astype(BF16)


def _gqa(qg, kg, vg, sink, batch, seq):
    tq = GQ_Q_TILE
    nq = seq // tq
    per_tile, per_seq = tq // WINDOW, seq // WINDOW
    cur = pl.BlockSpec((tq, 256), lambda b, i: (b * nq + i, 0))
    prev = pl.BlockSpec((WINDOW, 256), lambda b, i: (b * per_seq + jnp.maximum(i * per_tile - 1, 0), 0))
    nxt = pl.BlockSpec((WINDOW, 256), lambda b, i: (b * per_seq + jnp.minimum((i + 1) * per_tile, per_seq - 1), 0))
    return pl.pallas_call(
        _gqa_kernel,
        grid=(batch, nq),
        in_specs=[pl.BlockSpec(memory_space=pltpu.SMEM), cur, prev, cur, nxt, prev, cur, nxt],
        out_specs=cur,
        out_shape=jax.ShapeDtypeStruct((batch * seq, 256), BF16),
        compiler_params=pltpu.CompilerParams(dimension_semantics=("parallel", "parallel")),
        name="window_gqa",
    )(sink, qg, kg, kg, kg, vg, vg, vg)


def _out_kernel(alpha, ya_ref, yb_ref, yc_ref, yd_ref, x_ref, wout_ref, g_ref, b_ref, wrh_ref, wrl_ref, br_ref,
                x1_ref, aff_ref):
    mix = _dot(ya_ref[...], wout_ref[0:256, :])
    mix = mix + _dot(yb_ref[...], wout_ref[256:512, :])
    mix = mix + _dot(yc_ref[...], wout_ref[512:768, :])
    mix = mix + _dot(yd_ref[...], wout_ref[768:1024, :])
    x1 = _layer_norm(alpha * x_ref[...] + mix, g_ref[...], b_ref[...])
    x1_ref[...] = x1
    hi = x1.astype(BF16)
    lo = (x1 - hi.astype(F32)).astype(BF16)
    logits = _dot(hi, wrh_ref[...]) + _dot(lo, wrh_ref[...]) + _dot(hi, wrl_ref[...]) + br_ref[...]
    e = jnp.exp(logits - jnp.max(logits, axis=-1, keepdims=True))
    aff_ref[...] = e / jnp.sum(e, axis=-1, keepdims=True)


def _out(ya, yb, yc, yd, x, w, alpha):
    t = x.shape[0]
    tm = TOKEN_TILE
    row = lambda width: pl.BlockSpec((tm, width), lambda i: (i, 0))
    full = lambda a: pl.BlockSpec(a.shape, lambda i: (0,) * a.ndim)
    small = [w["w_out"], w["ln1_g"], w["ln1_b"], w["w_router_hi"], w["w_router_lo"], w["b_router"]]
    return pl.pallas_call(
        functools.partial(_out_kernel, alpha),
        grid=(t // tm,),
        in_specs=[row(256)] * 4 + [row(D_MODEL)] + [full(a) for a in small],
        out_specs=[row(D_MODEL), row(LANES)],
        out_shape=[jax.ShapeDtypeStruct((t, D_MODEL), F32), jax.ShapeDtypeStruct((t, LANES), F32)],
        compiler_params=pltpu.CompilerParams(dimension_semantics=("parallel",)),
        name="out_proj",
    )(ya, yb, yc, yd, x, *small)


def _route_kernel(cap, aff_ref, idx_ref, afft_ref, cnt_ref):
    seq = aff_ref.shape[0]
    nblk = seq // LANES
    def refine(it, thr_bits):
        cand = thr_bits | jnp.left_shift(jnp.int32(1), 30 - it)
        cand_f = pltpu.bitcast(cand, F32)[0:1, :]
        cnt = jnp.sum((aff_ref[...] >= cand_f).astype(F32), axis=0, keepdims=True)
        return jnp.where((cnt >= cap) & (cand >= MIN_NORMAL_BITS), cand, thr_bits)

    thr_bits = lax.fori_loop(0, 31, refine, jnp.zeros((SUBLANES, LANES), I32))
    thr = pltpu.bitcast(thr_bits, F32)[0:1, :]
    n_gt = jnp.sum((aff_ref[...] > thr).astype(F32), axis=0, keepdims=True)
    need = cap - n_gt

    tri = (lax.broadcasted_iota(I32, (LANES, LANES), 0) >= lax.broadcasted_iota(I32, (LANES, LANES), 1)).astype(BF16)
    off_eq = jnp.zeros((1, LANES), F32)
    off_sel = jnp.zeros((1, LANES), F32)
    for blk in range(nblk):
        a = aff_ref[blk * LANES:(blk + 1) * LANES, :]
        eq = a == thr
        eqf = eq.astype(F32)
        incl = _dot(tri, eqf.astype(BF16))
        rank = incl - eqf + off_eq
        off_eq = off_eq + incl[LANES - 1:LANES, :]
        sel = ((a > thr) | (eq & (rank < need))).astype(F32)
        incl = _dot(tri, sel.astype(BF16))
        cnt_ref[blk * LANES:(blk + 1) * LANES, :] = incl + off_sel
        off_sel = off_sel + incl[LANES - 1:LANES, :]

    slot = lax.broadcasted_iota(I32, (1, LANES), 1).astype(F32)
    rows = []
    for e in range(N_EXPERTS):
        col = jnp.broadcast_to(cnt_ref[:, e:e + 1], (seq, LANES))
        parts = [jnp.sum((col <= slot + float(q * LANES)).astype(F32), axis=0, keepdims=True)
                 for q in range(cap // LANES)]
        rows.append(jnp.concatenate(parts, axis=1))
    idx_ref[0] = jnp.concatenate(rows, axis=0).astype(I32)
    afft_ref[0] = aff_ref[...].T[:N_EXPERTS, :]


def _route(aff, batch, seq):
    cap = EC_CAPACITY_FACTOR * seq // N_EXPERTS
    return pl.pallas_call(
        functools.partial(_route_kernel, cap),
        grid=(batch,),
        in_specs=[pl.BlockSpec((seq, LANES), lambda b: (b, 0))],
        out_specs=[pl.BlockSpec((1, N_EXPERTS, cap), lambda b: (b, 0, 0)),
                   pl.BlockSpec((1, N_EXPERTS, seq), lambda b: (b, 0, 0))],
        out_shape=[jax.ShapeDtypeStruct((batch, N_EXPERTS, cap), I32),
                   jax.ShapeDtypeStruct((batch, N_EXPERTS, seq), F32)],
        scratch_shapes=[pltpu.VMEM((seq, LANES), F32)],
        compiler_params=pltpu.CompilerParams(dimension_semantics=("parallel",),
                                             vmem_limit_bytes=VMEM_LIMIT_BYTES),
        name="route",
    )(aff)


MOE_GATHER_UNROLL = 16
MOE_SCATTER_UNROLL = 8
MOE_FF_CHUNK = 256
MOE_INIT_ROWS = 512


def _moe_kernel(alpha, cap, idx_ref, gate_ref, x_hbm, wg_ref, wu_ref, wd_ref, z_hbm,
                x_vm, y_vm, tin, tout, sems):
    b = pl.program_id(0)
    e = pl.program_id(1)
    planes = D_MODEL // LANES
    pitch = cap + MOE_PITCH_PAD

    def x_copy():
        return pltpu.make_async_copy(x_hbm.at[b], x_vm, sems.at[0])

    def z_copy():
        return pltpu.make_async_copy(y_vm, z_hbm.at[b], sems.at[1])

    @pl.when(e == 0)
    def _():
        x_copy().start()
        x_copy().wait()

        def init(r, carry):
            rows = pl.ds(pl.multiple_of(r * MOE_INIT_ROWS, MOE_INIT_ROWS), MOE_INIT_ROWS)
            y_vm[rows, :] = alpha * x_vm[rows, :]
            return carry

        lax.fori_loop(0, x_vm.shape[0] // MOE_INIT_ROWS, init, 0)

    def gather(kk, carry):
        for u in range(MOE_GATHER_UNROLL):
            c = kk * MOE_GATHER_UNROLL + u
            tok = idx_ref[0, 0, c]
            slab = x_vm[pl.ds(pl.multiple_of(tok * planes, planes), planes), :]
            tin[pl.ds(c, planes, stride=pitch), :] = slab
        return carry

    lax.fori_loop(0, cap // MOE_GATHER_UNROLL, gather, 0)

    xs = jnp.concatenate([tin[p * pitch:p * pitch + cap, :] for p in range(planes)], axis=1).astype(BF16)
    out = jnp.zeros((cap, D_MODEL), F32)
    for f0 in range(0, EXPERT_FF, MOE_FF_CHUNK):
        hg = _dot(xs, wg_ref[0, :, f0:f0 + MOE_FF_CHUNK])
        hu = _dot(xs, wu_ref[0, :, f0:f0 + MOE_FF_CHUNK])
        hid = (hg * jax.nn.sigmoid(hg) * hu).astype(BF16)
        out = out + _dot(hid, wd_ref[0, f0:f0 + MOE_FF_CHUNK, :])
    for p in range(planes):
        tout[p * pitch:p * pitch + cap, :] = out[:, p * LANES:(p + 1) * LANES]

    def scatter(kk, carry):
        pending = []
        for u in range(MOE_SCATTER_UNROLL):
            c = kk * MOE_SCATTER_UNROLL + u
            tok = idx_ref[0, 0, c]
            gate = gate_ref[0, 0, tok]
            rows = pl.ds(pl.multiple_of(tok * planes, planes), planes)
            pending.append((rows, y_vm[rows, :] + gate * tout[pl.ds(c, planes, stride=pitch), :]))
        for rows, val in pending:
            y_vm[rows, :] = val
        return carry

    lax.fori_loop(0, cap // MOE_SCATTER_UNROLL, scatter, 0)

    @pl.when(e == pl.num_programs(1) - 1)
    def _():
        z_copy().start()
        z_copy().wait()


def _moe(x1, idx, afft, w, alpha, batch, seq):
    cap = idx.shape[-1]
    planes = D_MODEL // LANES
    pitch = cap + MOE_PITCH_PAD
    wspec = pl.BlockSpec((1, D_MODEL, EXPERT_FF), lambda b, e: (e, 0, 0))
    z = pl.pallas_call(
        functools.partial(_moe_kernel, alpha, cap),
        grid=(batch, N_EXPERTS),
        in_specs=[pl.BlockSpec((1, 1, cap), lambda b, e: (b * N_EXPERTS + e, 0, 0), memory_space=pltpu.SMEM),
                  pl.BlockSpec((1, 1, seq), lambda b, e: (b * N_EXPERTS + e, 0, 0), memory_space=pltpu.SMEM),
                  pl.BlockSpec(memory_space=pl.ANY),
                  wspec, wspec, pl.BlockSpec((1, EXPERT_FF, D_MODEL), lambda b, e: (e, 0, 0))],
        out_specs=pl.BlockSpec(memory_space=pl.ANY),
        out_shape=jax.ShapeDtypeStruct((batch, seq * planes, LANES), F32),
        scratch_shapes=[pltpu.VMEM((seq * planes, LANES), F32), pltpu.VMEM((seq * planes, LANES), F32),
                        pltpu.VMEM((planes * pitch, LANES), F32), pltpu.VMEM((planes * pitch, LANES), F32),
                        pltpu.SemaphoreType.DMA((2,))],
        compiler_params=pltpu.CompilerParams(dimension_semantics=("arbitrary", "arbitrary"),
                                             vmem_limit_bytes=VMEM_LIMIT_BYTES),
        name="moe",
    )(idx.reshape(batch * N_EXPERTS, 1, cap), afft.reshape(batch * N_EXPERTS, 1, seq),
      x1.reshape(batch, seq * planes, LANES), w["w_gate"], w["w_up"], w["w_down"])
    return z.reshape(batch * seq, D_MODEL)


def _final_norm_kernel(z_ref, g_ref, b_ref, o_ref):
    o_ref[...] = _layer_norm(z_ref[...], g_ref[...], b_ref[...])


def _final_norm(z, g, b):
    t = z.shape[0]
    tm = TOKEN_TILE
    return pl.pallas_call(
        _final_norm_kernel,
        grid=(t // tm,),
        in_specs=[pl.BlockSpec((tm, D_MODEL), lambda i: (i, 0)),
                  pl.BlockSpec((1, D_MODEL), lambda i: (0, 0)), pl.BlockSpec((1, D_MODEL), lambda i: (0, 0))],
        out_specs=pl.BlockSpec((tm, D_MODEL), lambda i: (i, 0)),
        out_shape=jax.ShapeDtypeStruct((t, D_MODEL), F32),
        compiler_params=pltpu.CompilerParams(dimension_semantics=("parallel",)),
        name="final_norm",
    )(z, g, b)


def _layer_weights(l, w_in, q_norm_g, w_uq, kv_norm_g, w_ukv, sconv_w, cconv_w, cconv_b, cnorm_g, cnorm_b, sink,
                   w_out, ln1_g, ln1_b, w_router, b_router, w_gate, w_up, w_down, ln2_g, ln2_b):
    splits = np.cumsum([0, MLA_Q_RANK, MLA_KV_RANK, MLA_ROPE, SC_WIDTH, SC_WIDTH, SC_WIDTH, CF_WIDTH, CF_WIDTH,
                        GQ_HEADS * GQ_HEAD_DIM, GQ_KV_HEADS * GQ_HEAD_DIM, GQ_KV_HEADS * GQ_HEAD_DIM])
    part = [w_in[l][:, splits[i]:splits[i + 1]] for i in range(11)]
    zcols = lambda n: jnp.zeros((D_MODEL, n), F32)
    dup = lambda a: jnp.concatenate([a[:, :GQ_HEAD_DIM], a[:, :GQ_HEAD_DIM], a[:, GQ_HEAD_DIM:], a[:, GQ_HEAD_DIM:]], 1)
    win = jnp.concatenate([part[0], part[1], zcols(MLA_NOPE), part[2], zcols(LANES - MLA_NOPE - MLA_ROPE),
                           part[3], part[4], part[5], part[6], part[7], part[8], dup(part[9]), dup(part[10])], axis=1)
    dqk = MLA_NOPE + MLA_ROPE
    wuq = jnp.pad(w_uq[l].reshape(MLA_Q_RANK, MLA_HEADS, dqk), ((0, 0), (0, 0), (0, LANES - dqk)))
    wkv = w_ukv[l].reshape(MLA_KV_RANK, MLA_HEADS, MLA_NOPE + MLA_V)
    wk = jnp.pad(wkv[:, :, :MLA_NOPE], ((0, 0), (0, 0), (0, LANES - MLA_NOPE))).reshape(MLA_KV_RANK, MLA_HEADS * LANES)
    wv = wkv[:, :, MLA_NOPE:].reshape(MLA_KV_RANK, MLA_HEADS * MLA_V)
    wr = jnp.pad(w_router[l], ((0, 0), (0, LANES - N_EXPERTS)))
    wr_hi = wr.astype(BF16)
    row = lambda a: a[l][None, :]
    return dict(
        w_in=win.astype(BF16),
        q_norm_g=row(q_norm_g), w_uq=wuq.reshape(MLA_Q_RANK, MLA_HEADS * LANES).astype(BF16),
        kv_norm_g=row(kv_norm_g), w_kv=jnp.concatenate([wk, wv], axis=1).astype(BF16),
        sconv_w=jnp.pad(sconv_w[l], ((0, SUBLANES - SC_KERNEL), (0, 0))),
        cconv_w=jnp.pad(cconv_w[l], ((0, 1), (0, 0))),
        cconv_b=row(cconv_b), cnorm_g=row(cnorm_g), cnorm_b=row(cnorm_b),
        sink=sink[l],
        w_out=w_out[l].astype(BF16), ln1_g=row(ln1_g), ln1_b=row(ln1_b),
        w_router_hi=wr_hi, w_router_lo=(wr - wr_hi.astype(F32)).astype(BF16),
        b_router=jnp.pad(b_router[l], (0, LANES - N_EXPERTS), constant_values=NEG_INF)[None, :],
        w_gate=w_gate[l].astype(BF16), w_up=w_up[l].astype(BF16), w_down=w_down[l].astype(BF16),
        ln2=(row(ln2_g), row(ln2_b)),
    )


def kernel(x, positions, w_in, q_norm_g, w_uq, kv_norm_g, w_ukv, sconv_w, cconv_w, cconv_b, cnorm_g, cnorm_b, sink,
           w_out, ln1_g, ln1_b, w_router, b_router, w_gate, w_up, w_down, ln2_g, ln2_b):
    batch, seq, _ = x.shape
    depth = w_in.shape[0]
    alpha = float((2 * depth) ** 0.25)
    params = (w_in, q_norm_g, w_uq, kv_norm_g, w_ukv, sconv_w, cconv_w, cconv_b, cnorm_g, cnorm_b, sink,
              w_out, ln1_g, ln1_b, w_router, b_router, w_gate, w_up, w_down, ln2_g, ln2_b)
    cos_tab, sin_tab = _rope_tables(positions)
    h = x.reshape(batch * seq, D_MODEL)
    pending_ln = None
    for l in range(depth):
        w = _layer_weights(l, *params)
        h, (qm, km, vm, scb, usc, ucf, qg, kg, vg) = _proj(h, pending_ln, w, cos_tab, sin_tab)
        yb, yc = _conv(scb, usc, ucf, w, batch, seq)
        ya = _mla(qm, km, vm, batch, seq)
        yd = _gqa(qg, kg, vg, w["sink"], batch, seq)
        x1, aff = _out(ya, yb, yc, yd, h, w, alpha)
        idx, afft = _route(aff, batch, seq)
        h = _moe(x1, idx, afft, w, alpha, batch, seq)
        pending_ln = w["ln2"]
    return _final_norm(h, *pending_ln).reshape(batch, seq, D_MODEL)
```
